```python
import jax, jax.numpy as jnp
from jax import lax
import numpy as np

D_MODEL = 2048
BATCH = 1
SEQ = 16384
DEPTH = 4

HEAD_DIM = 128
N_HEADS_DSA = 8
N_HEADS_MOBA = 8
KV_LATENT = 256
IDX_HEADS = 16
IDX_DIM = 64
DSA_TOPK_MAX = 256
MOBA_BLOCK = 256
MOBA_TOPK_MAX = 3
N_MEM = 256
N_HEADS_MEM = 4
D_FF = -(-8 * D_MODEL // (3 * 256)) * 256
ATTN_Q_BLOCK = 128
MOBA_Q_BLOCK = 64
DSA_WIDTH = N_HEADS_DSA * HEAD_DIM
MOBA_WIDTH = N_HEADS_MOBA * HEAD_DIM
MEM_WIDTH = N_HEADS_MEM * HEAD_DIM
SPLIT_SIZES = (DSA_WIDTH, KV_LATENT, IDX_HEADS * IDX_DIM, IDX_DIM, IDX_HEADS,
               MOBA_WIDTH, MOBA_WIDTH, MOBA_WIDTH, D_MODEL, D_MODEL)
IN_WIDTH = sum(SPLIT_SIZES)
ALPHA = (2.0 * DEPTH) ** 0.25
BETA = (8.0 * DEPTH) ** -0.25
LN_EPS = 1e-5

kernel_name = 'hybrid_dsa_moba_gated_deepnorm'


def layer_norm(x, g, b):
    xf = x.astype(jnp.float32)
    mu = xf.mean(-1, keepdims=True)
    var = jnp.square(xf - mu).mean(-1, keepdims=True)
    return ((xf - mu) * lax.rsqrt(var + LN_EPS) * g.astype(jnp.float32) + b.astype(jnp.float32)).astype(x.dtype)


def rms_norm(x, g):
    xf = x.astype(jnp.float32)
    return (xf * lax.rsqrt(jnp.mean(xf * xf, -1, keepdims=True) + LN_EPS) * g.astype(jnp.float32)).astype(x.dtype)


def alibi_slopes(n):
    return jnp.asarray([2.0 ** (-8.0 * (i + 1) / n) for i in range(n)], dtype=jnp.float32)


def dsa_attention(q, c_kv, q_idx, k_idx, w_idx, w_uk, w_uv):
    S = q.shape[0]
    topk = min(DSA_TOPK_MAX, S // 4)
    slopes = alibi_slopes(N_HEADS_DSA)
    q_lat = jnp.einsum('shd,hdc->shc', q, w_uk)
    key_pos = jnp.arange(S)
    idx_scale = IDX_DIM ** -0.5
    w_scale = IDX_HEADS ** -0.5

    def block(i):
        t0 = i * ATTN_Q_BLOCK
        qi = lax.dynamic_slice_in_dim(q_idx, t0, ATTN_Q_BLOCK, 0)
        wi = lax.dynamic_slice_in_dim(w_idx, t0, ATTN_Q_BLOCK, 0)
        ql = lax.dynamic_slice_in_dim(q_lat, t0, ATTN_Q_BLOCK, 0)
        qpos = t0 + jnp.arange(ATTN_Q_BLOCK)
        logits = jnp.einsum('thd,sd->ths', qi, k_idx, preferred_element_type=jnp.float32) * idx_scale
        score = jnp.einsum('th,ths->ts', wi.astype(jnp.float32) * w_scale, jax.nn.relu(logits))
        score = jnp.where(key_pos[None, :] <= qpos[:, None], score, -jnp.inf)
        _, sel = lax.top_k(score, topk)
        valid = sel <= qpos[:, None]
        c_sel = c_kv[sel]
        s = jnp.einsum('thc,tkc->thk', ql, c_sel, preferred_element_type=jnp.float32) * (HEAD_DIM ** -0.5)
        dist = (qpos[:, None] - sel).astype(jnp.float32)
        s = s - slopes[None, :, None] * dist[:, None, :]
        s = jnp.where(valid[:, None, :], s, -jnp.inf)
        p = jax.nn.softmax(s, axis=-1).astype(c_kv.dtype)
        return jnp.einsum('thk,tkc->thc', p, c_sel)

    o_lat = lax.map(block, jnp.arange(S // ATTN_Q_BLOCK)).reshape(S, N_HEADS_DSA, KV_LATENT)
    o = jnp.einsum('shc,hcd->shd', o_lat, w_uv)
    return o.reshape(S, DSA_WIDTH)


def moba_attention(q, k, v):
    S, H, Dh = q.shape
    n_blk = -(-S // MOBA_BLOCK)
    pad = n_blk * MOBA_BLOCK - S
    kb = jnp.pad(k, ((0, pad), (0, 0), (0, 0))).reshape(n_blk, MOBA_BLOCK, H, Dh).transpose(2, 0, 1, 3)
    vb = jnp.pad(v, ((0, pad), (0, 0), (0, 0))).reshape(n_blk, MOBA_BLOCK, H, Dh).transpose(2, 0, 1, 3)
    k_mean = kb.astype(jnp.float32).mean(axis=2)
    n_sel = min(MOBA_TOPK_MAX, n_blk - 1)
    n_past = n_sel * MOBA_BLOCK
    slopes = alibi_slopes(H)
    scale = Dh ** -0.5
    offs = jnp.arange(MOBA_BLOCK)
    blk_ids = jnp.arange(n_blk)
    head_ix = jnp.arange(H)[None, :, None]

    def block(i):
        t0 = i * MOBA_Q_BLOCK
        qi = lax.dynamic_slice_in_dim(q, t0, MOBA_Q_BLOCK, 0)
        qpos = t0 + jnp.arange(MOBA_Q_BLOCK)
        own = t0 // MOBA_BLOCK
        k_own = lax.dynamic_index_in_dim(kb, own, axis=1, keepdims=False)
        v_own = lax.dynamic_index_in_dim(vb, own, axis=1, keepdims=False)
        own_pos = own * MOBA_BLOCK + offs
        s_own = jnp.einsum('thd,hkd->thk', qi, k_own, preferred_element_type=jnp.float32) * scale
        s_own = s_own - slopes[None, :, None] * (qpos[:, None] - own_pos[None, :]).astype(jnp.float32)[:, None, :]
        s_own = jnp.where((own_pos[None, :] <= qpos[:, None])[:, None, :], s_own, -jnp.inf)
        if n_sel == 0:
            p = jax.nn.softmax(s_own, axis=-1).astype(v.dtype)
            return jnp.einsum('thk,hkd->thd', p, v_own)
        gate = jnp.einsum('thd,hnd->thn', qi.astype(jnp.float32), k_mean)
        gate = jnp.where(blk_ids[None, None, :] < own, gate, -jnp.inf)
        _, sel = lax.top_k(gate, n_sel)
        k_sel = kb[head_ix, sel]
        v_sel = vb[head_ix, sel].reshape(MOBA_Q_BLOCK, H, n_past, Dh)
        pos_sel = sel[..., None] * MOBA_BLOCK + offs
        s_sel = jnp.einsum('thd,thnkd->thnk', qi, k_sel, preferred_element_type=jnp.float32) * scale
        s_sel = s_sel - slopes[None, :, None, None] * (qpos[:, None, None, None] - pos_sel).astype(jnp.float32)
        s_sel = jnp.where((sel < own)[..., None], s_sel, -jnp.inf).reshape(MOBA_Q_BLOCK, H, n_past)
        s = jnp.concatenate([s_sel, s_own], axis=-1)
        p = jax.nn.softmax(s, axis=-1).astype(v.dtype)
        return (jnp.einsum('thk,thkd->thd', p[..., :n_past], v_sel)
                + jnp.einsum('thk,hkd->thd', p[..., n_past:], v_own))

    o = lax.map(block, jnp.arange(S // MOBA_Q_BLOCK))
    return o.reshape(S, H * Dh)


def hybrid_mixer(x, w_in, b_gate, kv_norm_g, idx_k_norm_g, idx_k_norm_b, w_uk, w_uv, w_o_dsa, w_o_moba, w_out):
    B, S, _ = x.shape
    points = np.cumsum(SPLIT_SIZES)[:-1].tolist()
    proj = x @ w_in
    q_dsa, c_kv, q_idx, k_idx, w_idx, q_m, k_m, v_m, ga, gb = jnp.split(proj, points, axis=-1)
    c_kv = rms_norm(c_kv, kv_norm_g)
    k_idx = layer_norm(k_idx, idx_k_norm_g, idx_k_norm_b)
    q_dsa = q_dsa.reshape(B, S, N_HEADS_DSA, HEAD_DIM)
    q_idx = q_idx.reshape(B, S, IDX_HEADS, IDX_DIM)
    o_a = jax.vmap(dsa_attention, in_axes=(0, 0, 0, 0, 0, None, None))(q_dsa, c_kv, q_idx, k_idx, w_idx, w_uk, w_uv)
    hm = (B, S, N_HEADS_MOBA, HEAD_DIM)
    o_b = jax.vmap(moba_attention)(q_m.reshape(hm), k_m.reshape(hm), v_m.reshape(hm))
    g_a = jax.nn.sigmoid(ga + b_gate[0])
    g_b = jax.nn.sigmoid(gb + b_gate[1])
    merged = g_a * (o_a @ w_o_dsa) + g_b * (o_b @ w_o_moba)
    return merged @ w_out


def mem_cross_attention(x, mem, w_q, w_kv, w_o):
    B, S, _ = x.shape
    q = (x @ w_q).reshape(B, S, N_HEADS_MEM, HEAD_DIM)
    k, v = jnp.split(mem @ w_kv, 2, axis=-1)
    k = k.reshape(B, N_MEM, N_HEADS_MEM, HEAD_DIM)
    v = v.reshape(B, N_MEM, N_HEADS_MEM, HEAD_DIM)
    s = jnp.einsum('bshd,bmhd->bhsm', q, k, preferred_element_type=jnp.float32) * (HEAD_DIM ** -0.5)
    p = jax.nn.softmax(s, axis=-1).astype(x.dtype)
    o = jnp.einsum('bhsm,bmhd->bshd', p, v).reshape(B, S, MEM_WIDTH)
    return o @ w_o


def swiglu(x, w_in, w_out):
    g, u = jnp.split(x @ w_in, 2, axis=-1)
    return (jax.nn.silu(g) * u) @ w_out


def setup_inputs(seed: int = 0) -> dict:
    key = jax.random.key(seed)
    ks = jax.random.split(key, 20)
    f32 = jnp.float32

    def nrm(k, shape, scale):
        return jax.random.normal(k, shape, f32) * scale

    v_off = sum(SPLIT_SIZES[:7])
    col_scale = jnp.ones((IN_WIDTH,), f32).at[v_off:v_off + MOBA_WIDTH].set(BETA)
    kv_scale = jnp.ones((2 * MEM_WIDTH,), f32).at[MEM_WIDTH:].set(BETA)
    return {
        'x': nrm(ks[0], (BATCH, SEQ, D_MODEL), 1.0),
        'mem': nrm(ks[1], (BATCH, N_MEM, D_MODEL), 1.0),
        'w_in': nrm(ks[2], (DEPTH, D_MODEL, IN_WIDTH), D_MODEL ** -0.5) * col_scale,
        'b_gate': nrm(ks[3], (DEPTH, 2, D_MODEL), 0.02),
        'kv_norm_g': 1.0 + nrm(ks[4], (DEPTH, KV_LATENT), 0.02),
        'idx_k_norm_g': 1.0 + nrm(ks[5], (DEPTH, IDX_DIM), 0.02),
        'idx_k_norm_b': nrm(ks[6], (DEPTH, IDX_DIM), 0.02),
        'w_uk': nrm(ks[7], (DEPTH, N_HEADS_DSA, HEAD_DIM, KV_LATENT), HEAD_DIM ** -0.5),
        'w_uv': nrm(ks[8], (DEPTH, N_HEADS_DSA, KV_LATENT, HEAD_DIM), BETA * KV_LATENT ** -0.5),
        'w_o_dsa': nrm(ks[9], (DEPTH, DSA_WIDTH, D_MODEL), BETA * DSA_WIDTH ** -0.5),
        'w_o_moba': nrm(ks[10], (DEPTH, MOBA_WIDTH, D_MODEL), BETA * MOBA_WIDTH ** -0.5),
        'w_out': nrm(ks[11], (DEPTH, D_MODEL, D_MODEL), BETA * D_MODEL ** -0.5),
        'w_q_mem': nrm(ks[12], (DEPTH, D_MODEL, MEM_WIDTH), D_MODEL ** -0.5),
        'w_kv_mem': nrm(ks[13], (DEPTH, D_MODEL, 2 * MEM_WIDTH), D_MODEL ** -0.5) * kv_scale,
        'w_o_mem': nrm(ks[14], (DEPTH, MEM_WIDTH, D_MODEL), BETA * MEM_WIDTH ** -0.5),
        'w_ffn_in': nrm(ks[15], (DEPTH, D_MODEL, 2 * D_FF), BETA * D_MODEL ** -0.5),
        'w_ffn_out': nrm(ks[16], (DEPTH, D_FF, D_MODEL), BETA * D_FF ** -0.5),
        'ln_g': 1.0 + nrm(ks[17], (DEPTH, 3, D_MODEL), 0.02),
        'ln_b': nrm(ks[18], (DEPTH, 3, D_MODEL), 0.02),
    }


def reference(x, mem, w_in, b_gate, kv_norm_g, idx_k_norm_g, idx_k_norm_b, w_uk, w_uv, w_o_dsa, w_o_moba,
              w_out, w_q_mem, w_kv_mem, w_o_mem, w_ffn_in, w_ffn_out, ln_g, ln_b):
    for l in range(DEPTH):
        y = hybrid_mixer(x, w_in[l], b_gate[l], kv_norm_g[l], idx_k_norm_g[l], idx_k_norm_b[l], w_uk[l], w_uv[l],
                         w_o_dsa[l], w_o_moba[l], w_out[l])
        x = layer_norm(ALPHA * x + y, ln_g[l, 0], ln_b[l, 0])
        y = mem_cross_attention(x, mem, w_q_mem[l], w_kv_mem[l], w_o_mem[l])
        x = layer_norm(ALPHA * x + y, ln_g[l, 1], ln_b[l, 1])
        y = swiglu(x, w_ffn_in[l], w_ffn_out[l])
        x = layer_norm(ALPHA * x + y, ln_g[l, 2], ln_b[l, 2])
    return x
```

```python
import functools
import math

import numpy as np
import jax
import jax.numpy as jnp
from jax import lax
from jax.experimental import pallas as pl
from jax.experimental.pallas import tpu as pltpu

HEAD_DIM = 128
N_HEADS_DSA = 8
N_HEADS_MOBA = 8
KV_LATENT = 256
IDX_HEADS = 16
IDX_DIM = 64
DSA_TOPK_MAX = 256
MOBA_BLOCK = 256
MOBA_TOPK_MAX = 3
N_HEADS_MEM = 4
TRUNK_DEPTH = 4
ALPHA = (2.0 * TRUNK_DEPTH) ** 0.25
LN_EPS = 1e-5

LOG2E = math.log2(math.e)
INT_MIN = -(2 ** 31)
NEG_BIG = -1e30
MOBA_MASK = -float(2 ** 30)
VMEM_LIMIT_V7X = 56 * 1024 * 1024
LANES = 128

F32 = jnp.float32
BF16 = jnp.bfloat16


def _alibi_slopes(n):
    return [2.0 ** (-8.0 * (i + 1) / n) for i in range(n)]


def _params(sem):
    return pltpu.CompilerParams(dimension_semantics=sem, vmem_limit_bytes=VMEM_LIMIT_V7X)


def _resident(shape):
    nd = len(shape)
    return pl.BlockSpec(shape, lambda *_: (0,) * nd, pipeline_mode=pl.Buffered(1))


def _layer_norm(z, g, b):
    mu = jnp.mean(z, axis=-1, keepdims=True)
    zc = z - mu
    var = jnp.mean(zc * zc, axis=-1, keepdims=True)
    return zc * lax.rsqrt(var + LN_EPS) * g + b


def _dot(a, b):
    return jnp.dot(a, b, preferred_element_type=F32)


def _dot_nt(a, b):
    return lax.dot_general(a, b, (((1,), (1,)), ((), ())), preferred_element_type=F32)


def _proj_call(body, x, w, extras, extra_specs, out_shape, out_specs, tm, tn, name):
    m, k = x.shape
    n = w.shape[1]
    grid = (n // tn, m // tm)
    in_specs = [pl.BlockSpec((tm, k), lambda j, i: (i, 0)),
                pl.BlockSpec((k, tn), lambda j, i: (0, j))] + list(extra_specs)
    return pl.pallas_call(
        body, grid=grid, in_specs=in_specs, out_specs=out_specs, out_shape=out_shape,
        compiler_params=_params(("arbitrary", "arbitrary")), name=name,
    )(x, w, *extras)


def _qlat_body(x_ref, w_ref, wuk_ref, o_ref):
    acc = _dot(x_ref[...], w_ref[...])
    scale = HEAD_DIM ** -0.5 * LOG2E
    for h in range(N_HEADS_DSA):
        qh = acc[:, h * HEAD_DIM:(h + 1) * HEAD_DIM].astype(BF16)
        o_ref[h] = (_dot(qh, wuk_ref[h]) * scale).astype(o_ref.dtype)


def _ckv_body(x_ref, w_ref, g_ref, o_ref):
    acc = _dot(x_ref[...], w_ref[...])
    ms = jnp.mean(acc * acc, axis=-1, keepdims=True)
    o_ref[...] = (acc * lax.rsqrt(ms + LN_EPS) * g_ref[...]).astype(o_ref.dtype)


def _qidx_body(x_ref, w_ref, o_ref):
    acc = _dot(x_ref[...], w_ref[...])
    for g in range(IDX_HEADS // 2):
        o_ref[g] = acc[:, g * LANES:(g + 1) * LANES].astype(o_ref.dtype)


def _kw_body(x_ref, w_ref, g_ref, b_ref, o_ref):
    acc = _dot(x_ref[...], w_ref[...])
    lane = lax.broadcasted_iota(jnp.int32, acc.shape, 1)
    is_k = lane < IDX_DIM
    mu = jnp.sum(jnp.where(is_k, acc, 0.0), axis=-1, keepdims=True) * (1.0 / IDX_DIM)
    zc = jnp.where(is_k, acc - mu, 0.0)
    var = jnp.sum(zc * zc, axis=-1, keepdims=True) * (1.0 / IDX_DIM)
    kn = zc * lax.rsqrt(var + LN_EPS) * g_ref[...] + b_ref[...]
    w_scale = IDX_DIM ** -0.5 * IDX_HEADS ** -0.5
    o_ref[...] = jnp.where(is_k, kn, acc * w_scale)


def _qkv_body(x_ref, w_ref, o_ref):
    acc = _dot(x_ref[...], w_ref[...])
    scale = jnp.where(pl.program_id(0) == 0, HEAD_DIM ** -0.5, 1.0).astype(F32)
    o_ref[...] = (acc * scale).astype(o_ref.dtype)


def _gate_body(x_ref, w_ref, b_ref, o_ref):
    acc = _dot(x_ref[...], w_ref[...]) + b_ref[...]
    o_ref[...] = jax.nn.sigmoid(acc).astype(o_ref.dtype)


def _plain_body(x_ref, w_ref, o_ref):
    o_ref[...] = _dot(x_ref[...], w_ref[...]).astype(o_ref.dtype)


def _dsa_kernel(qlat_ref, qidx_ref, kw_ref, ckv_ref, kte_ref, kto_ref, wuv_ref, o_ref,
                keys_ref, acc_ref, m_ref, l_ref, p_ref, alpha_ref, *, T, KS, S, topk):
    H = N_HEADS_DSA
    R = H * T
    G = IDX_HEADS // 2
    nsub = KS // LANES
    i = pl.program_id(0)
    q0 = i * T
    n_chunks = (q0 + T + KS - 1) // KS
    int_min = jnp.int32(INT_MIN)

    qi = qidx_ref[...].reshape(G * T, LANES)
    row_ks = lax.broadcasted_iota(jnp.int32, (T, KS), 0)
    lane_ks = lax.broadcasted_iota(jnp.int32, (T, KS), 1)

    def phase_a(c, carry):
        off = pl.multiple_of(c * KS, KS)
        le = _dot(qi, kte_ref[:, pl.ds(off, KS)])
        lo = _dot(qi, kto_ref[:, pl.ds(off, KS)])
        sc = jnp.zeros((T, KS), F32)
        for g in range(G):
            we = kw_ref[:, IDX_DIM + 2 * g:IDX_DIM + 2 * g + 1]
            wo = kw_ref[:, IDX_DIM + 2 * g + 1:IDX_DIM + 2 * g + 2]
            sc = sc + we * jnp.maximum(le[g * T:(g + 1) * T], 0.0)
            sc = sc + wo * jnp.maximum(lo[g * T:(g + 1) * T], 0.0)
        bits = pltpu.bitcast(sc, jnp.int32)
        key = bits ^ ((bits >> 31) & jnp.int32(0x7FFFFFFF))
        key = jnp.where(off + lane_ks <= q0 + row_ks, key, int_min)
        keys_ref[:, pl.ds(off, KS)] = key
        return carry

    lax.fori_loop(0, n_chunks, phase_a, 0)

    def count_ge(cand):
        def body(c, cnt):
            off = pl.multiple_of(c * KS, KS)
            k = keys_ref[:, pl.ds(off, KS)]
            for a in range(nsub):
                cnt = cnt + jnp.where(k[:, a * LANES:(a + 1) * LANES] >= cand, 1.0, 0.0)
            return cnt
        cnt = lax.fori_loop(0, n_chunks, body, jnp.zeros((T, LANES), F32))
        return jnp.broadcast_to(jnp.sum(cnt, axis=1, keepdims=True), (T, LANES))

    def bis_step(b, carry):
        u, cnt_u = carry
        u_try = u | jnp.left_shift(jnp.int32(1), 31 - b)
        c = count_ge(u_try ^ int_min)
        take = c >= float(topk)
        return jnp.where(take, u_try, u), jnp.where(take, c, cnt_u)

    u0 = jnp.zeros((T, LANES), jnp.int32)
    c0 = jnp.full((T, LANES), float(S), F32)
    u, cnt_u = lax.fori_loop(0, 32, bis_step, (u0, c0))
    tau = u ^ int_min
    excess = jnp.logical_and(cnt_u > float(topk), u != 0)
    any_excess = jnp.max(jnp.where(excess, 1.0, 0.0)) > 0.0

    @pl.when(any_excess)
    def _break_ties():
        need = float(topk) - count_ge(tau + 1)
        lane_sub = lax.broadcasted_iota(jnp.int32, (T, LANES), 1)

        def count_tie_below(jt):
            def body(c, cnt):
                off = pl.multiple_of(c * KS, KS)
                k = keys_ref[:, pl.ds(off, KS)]
                for a in range(nsub):
                    hit = jnp.logical_and(k[:, a * LANES:(a + 1) * LANES] == tau,
                                          off + a * LANES + lane_sub < jt)
                    cnt = cnt + jnp.where(hit, 1.0, 0.0)
                return cnt
            cnt = lax.fori_loop(0, n_chunks, body, jnp.zeros((T, LANES), F32))
            return jnp.broadcast_to(jnp.sum(cnt, axis=1, keepdims=True), (T, LANES))

        nbits = max(1, (S - 1).bit_length())

        def j_step(b, j):
            j_try = j | jnp.left_shift(jnp.int32(1), nbits - 1 - b)
            return jnp.where(count_tie_below(j_try) < need, j_try, j)

        j_keep = lax.fori_loop(0, nbits, j_step, jnp.zeros((T, LANES), jnp.int32))

        def demote(c, carry):
            off = pl.multiple_of(c * KS, KS)
            for a in range(nsub):
                sl = pl.ds(off + a * LANES, LANES)
                k = keys_ref[:, sl]
                drop = jnp.logical_and(jnp.logical_and(k == tau, excess),
                                       off + a * LANES + lane_sub > j_keep)
                keys_ref[:, sl] = jnp.where(drop, tau - 1, k)
            return carry

        lax.fori_loop(0, n_chunks, demote, 0)

    tau_sel = jnp.maximum(tau, int_min + 1)

    ql = qlat_ref[...].reshape(R, KV_LATENT)
    m_ref[...] = jnp.full((R, 1), NEG_BIG, F32)
    l_ref[...] = jnp.zeros((R, 1), F32)
    acc_ref[...] = jnp.zeros((R, KV_LATENT), F32)
    slopes = _alibi_slopes(H)
    lane_row = lax.broadcasted_iota(jnp.int32, (1, KS), 1)

    def phase_c(c, carry):
        off = pl.multiple_of(c * KS, KS)
        kv = ckv_ref[pl.ds(off, KS), :]
        s = _dot_nt(ql, kv)
        k = keys_ref[:, pl.ds(off, KS)]
        mb = jnp.concatenate(
            [jnp.where(k[:, a * LANES:(a + 1) * LANES] >= tau_sel, 0.0, NEG_BIG) for a in range(nsub)],
            axis=1)
        rel = (off + lane_row - q0).astype(F32)
        for h in range(H):
            rows = slice(h * T, (h + 1) * T)
            sh = s[rows] + (slopes[h] * LOG2E) * rel + mb
            m_prev = m_ref[rows]
            m_new = jnp.maximum(m_prev, jnp.max(sh, axis=1, keepdims=True))
            p = jnp.exp2(sh - m_new)
            alpha = jnp.exp2(m_prev - m_new)
            l_ref[rows] = alpha * l_ref[rows] + jnp.sum(p, axis=1, keepdims=True)
            m_ref[rows] = m_new
            alpha_ref[rows] = alpha
            p_ref[rows] = p.astype(BF16)
        acc_ref[...] = acc_ref[...] * alpha_ref[...] + _dot(p_ref[...], kv)
        return carry

    lax.fori_loop(0, n_chunks, phase_c, 0)

    for h in range(H):
        rows = slice(h * T, (h + 1) * T)
        oh = (acc_ref[rows] / l_ref[rows]).astype(BF16)
        o_ref[:, h * HEAD_DIM:(h + 1) * HEAD_DIM] = _dot(oh, wuv_ref[h]).astype(o_ref.dtype)


def _dsa_attention(qlat, qidx, kw, ckv, kte, kto, wuv):
    H = N_HEADS_DSA
    S = ckv.shape[0]
    T, KS = 128, 512
    topk = min(DSA_TOPK_MAX, S // 4)
    R = H * T
    kern = functools.partial(_dsa_kernel, T=T, KS=KS, S=S, topk=topk)
    return pl.pallas_call(
        kern, grid=(S // T,),
        in_specs=[pl.BlockSpec((H, T, KV_LATENT), lambda i: (0, i, 0)),
                  pl.BlockSpec((IDX_HEADS // 2, T, LANES), lambda i: (0, i, 0)),
                  pl.BlockSpec((T, LANES), lambda i: (i, 0)),
                  _resident(ckv.shape), _resident(kte.shape), _resident(kto.shape),
                  _resident(wuv.shape)],
        out_specs=pl.BlockSpec((T, H * HEAD_DIM), lambda i: (i, 0)),
        out_shape=jax.ShapeDtypeStruct((S, H * HEAD_DIM), BF16),
        scratch_shapes=[pltpu.VMEM((T, S), jnp.int32),
                        pltpu.VMEM((R, KV_LATENT), F32),
                        pltpu.VMEM((R, 1), F32),
                        pltpu.VMEM((R, 1), F32),
                        pltpu.VMEM((R, KS), BF16),
                        pltpu.VMEM((R, 1), F32)],
        compiler_params=_params(("arbitrary",)), name="dsa_attention",
    )(qlat, qidx, kw, ckv, kte, kto, wuv)


def _moba_kernel(slope_ref, q_ref, kaug_ref, v_ref, o_ref, kmean_ref, qaug_ref, *, Tq, S, n_sel):
    B = MOBA_BLOCK
    NB = S // B
    h = pl.program_id(0)
    i = pl.program_id(1)
    q0 = i * Tq

    @pl.when(i == 0)
    def _block_means():
        kmean_ref[...] = jnp.zeros(kmean_ref.shape, F32)
        for b in range(NB):
            kb = kaug_ref[b * B:(b + 1) * B, 0:HEAD_DIM].astype(F32)
            kmean_ref[b:b + 1, :] = jnp.mean(kb, axis=0, keepdims=True)

    q = q_ref[...]
    gate = _dot_nt(q, kmean_ref[...].astype(BF16))
    row = lax.broadcasted_iota(jnp.int32, (Tq, LANES), 0)
    lane = lax.broadcasted_iota(jnp.int32, (Tq, LANES), 1)
    lane_f = lane.astype(F32)
    own = lax.shift_right_arithmetic(q0 + row, jnp.int32(B.bit_length() - 1))
    neg_inf = jnp.float32(-jnp.inf)
    g = jnp.where(lane < own, gate, neg_inf)
    selb = jnp.full((Tq, LANES), MOBA_MASK, F32)
    for _ in range(n_sel):
        mx = jnp.max(g, axis=1, keepdims=True)
        first = jnp.min(jnp.where(g == mx, lane_f, float(LANES)), axis=1, keepdims=True)
        pick = lane_f == first
        selb = jnp.where(jnp.logical_and(pick, mx > neg_inf), 0.0, selb)
        g = jnp.where(pick, neg_inf, g)
    selb = jnp.where(lane == own, 0.0, selb)
    slope = slope_ref[h]
    aug = jnp.where(lane < NB, selb,
                    jnp.where(lane == NB, slope * LANES, jnp.where(lane == NB + 1, slope, 0.0)))
    qaug_ref[:, 0:HEAD_DIM] = q
    qaug_ref[:, HEAD_DIM:2 * HEAD_DIM] = aug.astype(BF16)

    row_b = lax.broadcasted_iota(jnp.int32, (Tq, B), 0)
    lane_b = lax.broadcasted_iota(jnp.int32, (Tq, B), 1)

    def block(j, carry, diag):
        m_prev, l_prev, acc = carry
        off = pl.multiple_of(j * B, B)
        kj = kaug_ref[pl.ds(off, B), :]
        vj = v_ref[pl.ds(off, B), :]
        s = _dot_nt(qaug_ref[...], kj)
        if diag:
            s = jnp.where(off + lane_b <= q0 + row_b, s, NEG_BIG)
        m_new = jnp.maximum(m_prev, jnp.max(s, axis=1, keepdims=True))
        p = jnp.exp(s - m_new)
        alpha = jnp.exp(m_prev - m_new)
        l_new = alpha * l_prev + jnp.sum(p, axis=1, keepdims=True)
        acc = alpha * acc + _dot(p.astype(BF16), vj)
        return m_new, l_new, acc

    carry = (jnp.full((Tq, 1), NEG_BIG, F32), jnp.zeros((Tq, 1), F32), jnp.zeros((Tq, HEAD_DIM), F32))
    first_diag = q0 // B
    carry = lax.fori_loop(0, first_diag, functools.partial(block, diag=False), carry)
    for d in range(Tq // B):
        carry = block(first_diag + d, carry, True)
    _, l_fin, acc = carry
    o_ref[...] = (acc / l_fin).astype(o_ref.dtype)


def _moba_attention(qkv, kaug, slopes):
    H = N_HEADS_MOBA
    S = qkv.shape[0]
    Tq = 512
    NB = S // MOBA_BLOCK
    n_sel = min(MOBA_TOPK_MAX, NB - 1)
    kern = functools.partial(_moba_kernel, Tq=Tq, S=S, n_sel=n_sel)
    return pl.pallas_call(
        kern, grid=(H, S // Tq),
        in_specs=[pl.BlockSpec(memory_space=pltpu.SMEM),
                  pl.BlockSpec((Tq, HEAD_DIM), lambda h, i: (i, h)),
                  pl.BlockSpec((S, 2 * HEAD_DIM), lambda h, i: (0, h)),
                  pl.BlockSpec((S, HEAD_DIM), lambda h, i: (0, 2 * H + h))],
        out_specs=pl.BlockSpec((Tq, HEAD_DIM), lambda h, i: (i, h)),
        out_shape=jax.ShapeDtypeStruct((S, H * HEAD_DIM), BF16),
        scratch_shapes=[pltpu.VMEM((LANES, HEAD_DIM), F32),
                        pltpu.VMEM((Tq, 2 * HEAD_DIM), BF16)],
        compiler_params=_params(("arbitrary", "arbitrary")), name="moba_attention",
    )(slopes, qkv, kaug, qkv)


def _merge_kernel(oa_ref, ob_ref, g_ref, x_ref, woa_ref, wob_ref, wout_ref, lng_ref, lnb_ref,
                  xo_ref, xbo_ref):
    d = x_ref.shape[1]
    a = _dot(oa_ref[...], woa_ref[...])
    b = _dot(ob_ref[...], wob_ref[...])
    merged = g_ref[:, 0:d].astype(F32) * a + g_ref[:, d:2 * d].astype(F32) * b
    y = _dot(merged.astype(BF16), wout_ref[...])
    xn = _layer_norm(ALPHA * x_ref[...] + y, lng_ref[...], lnb_ref[...])
    xo_ref[...] = xn
    xbo_ref[...] = xn.astype(BF16)


def _merge(oa, ob, gates, x, woa, wob, wout, lng, lnb):
    S, D = x.shape
    tm = 256
    row = lambda w: pl.BlockSpec((tm, w), lambda i: (i, 0))
    return pl.pallas_call(
        _merge_kernel, grid=(S // tm,),
        in_specs=[row(oa.shape[1]), row(ob.shape[1]), row(2 * D), row(D),
                  _resident(woa.shape), _resident(wob.shape), _resident(wout.shape),
                  _resident(lng.shape), _resident(lnb.shape)],
        out_specs=[row(D), row(D)],
        out_shape=[jax.ShapeDtypeStruct((S, D), F32), jax.ShapeDtypeStruct((S, D), BF16)],
        compiler_params=_params(("arbitrary",)), name="merge_out_ln",
    )(oa, ob, gates, x, woa, wob, wout, lng, lnb)


def _mem_kernel(xb_ref, x_ref, wq_ref, kv_ref, wo_ref, lng_ref, lnb_ref, xo_ref, xbo_ref):
    H = N_HEADS_MEM
    W = H * HEAD_DIM
    q = (_dot(xb_ref[...], wq_ref[...]) * (HEAD_DIM ** -0.5)).astype(BF16)
    outs = []
    for h in range(H):
        cols = slice(h * HEAD_DIM, (h + 1) * HEAD_DIM)
        s = _dot_nt(q[:, cols], kv_ref[:, cols])
        m = jnp.max(s, axis=1, keepdims=True)
        p = jnp.exp(s - m)
        l = jnp.sum(p, axis=1, keepdims=True)
        o = _dot(p.astype(BF16), kv_ref[:, W + h * HEAD_DIM:W + (h + 1) * HEAD_DIM])
        outs.append((o / l).astype(BF16))
    y = _dot(jnp.concatenate(outs, axis=1), wo_ref[...])
    xn = _layer_norm(ALPHA * x_ref[...] + y, lng_ref[...], lnb_ref[...])
    xo_ref[...] = xn
    xbo_ref[...] = xn.astype(BF16)


def _mem_attention(xb, x, wq, kv, wo, lng, lnb):
    S, D = x.shape
    tm = 512
    row = lambda w: pl.BlockSpec((tm, w), lambda i: (i, 0))
    return pl.pallas_call(
        _mem_kernel, grid=(S // tm,),
        in_specs=[row(D), row(D), _resident(wq.shape), _resident(kv.shape), _resident(wo.shape),
                  _resident(lng.shape), _resident(lnb.shape)],
        out_specs=[row(D), row(D)],
        out_shape=[jax.ShapeDtypeStruct((S, D), F32), jax.ShapeDtypeStruct((S, D), BF16)],
        compiler_params=_params(("arbitrary",)), name="mem_attn_ln",
    )(xb, x, wq, kv, wo, lng, lnb)


def _ffn_kernel(xb_ref, x_ref, wg_ref, wu_ref, wd_ref, lng_ref, lnb_ref, xo_ref, xbo_ref, acc_ref):
    c = pl.program_id(1)

    @pl.when(c == 0)
    def _():
        acc_ref[...] = jnp.zeros(acc_ref.shape, F32)

    xb = xb_ref[...]
    g = _dot(xb, wg_ref[...])
    u = _dot(xb, wu_ref[...])
    hidden = (g * jax.nn.sigmoid(g) * u).astype(BF16)
    acc_ref[...] += _dot(hidden, wd_ref[...])

    @pl.when(c == pl.num_programs(1) - 1)
    def _():
        xn = _layer_norm(ALPHA * x_ref[...] + acc_ref[...], lng_ref[...], lnb_ref[...])
        xo_ref[...] = xn
        xbo_ref[...] = xn.astype(BF16)


def _ffn(xb, x, w_in, w_out, lng, lnb):
    S, D = x.shape
    F = w_out.shape[0]
    tm, tf = 512, 512
    nf = F // tf
    row = lambda w: pl.BlockSpec((tm, w), lambda i, c: (i, 0))
    return pl.pallas_call(
        _ffn_kernel, grid=(S // tm, nf),
        in_specs=[row(D), row(D),
                  pl.BlockSpec((D, tf), lambda i, c: (0, c)),
                  pl.BlockSpec((D, tf), lambda i, c: (0, nf + c)),
                  pl.BlockSpec((tf, D), lambda i, c: (c, 0)),
                  _resident(lng.shape), _resident(lnb.shape)],
        out_specs=[row(D), row(D)],
        out_shape=[jax.ShapeDtypeStruct((S, D), F32), jax.ShapeDtypeStruct((S, D), BF16)],
        scratch_shapes=[pltpu.VMEM((tm, D), F32)],
        compiler_params=_params(("arbitrary", "arbitrary")), name="swiglu_ffn_ln",
    )(xb, x, w_in, w_in, w_out, lng, lnb)


def _mixer_branches(xb, lw, kaug_tail, slopes_moba):
    S, D = xb.shape
    H = N_HEADS_DSA
    dsa_w, moba_w = H * HEAD_DIM, N_HEADS_MOBA * HEAD_DIM
    idx_w = IDX_HEADS * IDX_DIM
    o_q = 0
    o_c = o_q + dsa_w
    o_qi = o_c + KV_LATENT
    o_ki = o_qi + idx_w
    o_qm = o_ki + IDX_DIM + IDX_HEADS
    o_ga = o_qm + 3 * moba_w
    w_in = lw["w_in"]
    tm = 512

    qlat = _proj_call(
        _qlat_body, xb, w_in[:, o_q:o_c], [lw["w_uk"]], [_resident(lw["w_uk"].shape)],
        jax.ShapeDtypeStruct((H, S, KV_LATENT), BF16),
        pl.BlockSpec((H, tm, KV_LATENT), lambda j, i: (0, i, 0)), tm, dsa_w, "proj_qlat")
    ckv = _proj_call(
        _ckv_body, xb, w_in[:, o_c:o_qi], [lw["kv_norm_g"]], [_resident(lw["kv_norm_g"].shape)],
        jax.ShapeDtypeStruct((S, KV_LATENT), BF16),
        pl.BlockSpec((tm, KV_LATENT), lambda j, i: (i, 0)), tm, KV_LATENT, "proj_ckv")
    qidx = _proj_call(
        _qidx_body, xb, w_in[:, o_qi:o_ki], [], [],
        jax.ShapeDtypeStruct((IDX_HEADS // 2, S, LANES), BF16),
        pl.BlockSpec((IDX_HEADS // 2, tm, LANES), lambda j, i: (0, i, 0)), tm, idx_w, "proj_qidx")
    kw_pad = LANES - IDX_DIM - IDX_HEADS
    w_kw = jnp.pad(w_in[:, o_ki:o_qm], ((0, 0), (0, kw_pad)))
    kw = _proj_call(
        _kw_body, xb, w_kw, [lw["idx_g"], lw["idx_b"]],
        [_resident(lw["idx_g"].shape), _resident(lw["idx_b"].shape)],
        jax.ShapeDtypeStruct((S, LANES), F32),
        pl.BlockSpec((tm, LANES), lambda j, i: (i, 0)), tm, LANES, "proj_kidx_widx")
    qkv = _proj_call(
        _qkv_body, xb, w_in[:, o_qm:o_ga], [], [],
        jax.ShapeDtypeStruct((S, 3 * moba_w), BF16),
        pl.BlockSpec((tm, moba_w), lambda j, i: (i, j)), tm, moba_w, "proj_qkv_moba")
    gates = _proj_call(
        _gate_body, xb, w_in[:, o_ga:], [lw["b_gate"]], [pl.BlockSpec((1, 1024), lambda j, i: (0, j))],
        jax.ShapeDtypeStruct((S, 2 * D), BF16),
        pl.BlockSpec((tm, 1024), lambda j, i: (i, j)), tm, 1024, "proj_gates")

    kt = kw[:, :IDX_DIM].T.astype(BF16)
    zeros = jnp.zeros_like(kt)
    kte = jnp.concatenate([kt, zeros], axis=0)
    kto = jnp.concatenate([zeros, kt], axis=0)
    o_a = _dsa_attention(qlat, qidx, kw, ckv, kte, kto, lw["w_uv"])

    k3 = qkv[:, moba_w:2 * moba_w].reshape(S, N_HEADS_MOBA, HEAD_DIM)
    kaug = jnp.concatenate(
        [k3, jnp.broadcast_to(kaug_tail[:, None, :], (S, N_HEADS_MOBA, HEAD_DIM))], axis=-1
    ).reshape(S, N_HEADS_MOBA * 2 * HEAD_DIM)
    o_b = _moba_attention(qkv, kaug, slopes_moba)
    return o_a, o_b, gates


def _layer(carry, lw, mem_b, kaug_tail, slopes_moba):
    x, xb = carry
    o_a, o_b, gates = _mixer_branches(xb, lw, kaug_tail, slopes_moba)
    x, xb = _merge(o_a, o_b, gates, x, lw["w_o_dsa"], lw["w_o_moba"], lw["w_out"],
                   lw["ln_g"][0:1], lw["ln_b"][0:1])

    kv_mem = _proj_call(
        _plain_body, mem_b, lw["w_kv_mem"], [], [],
        jax.ShapeDtypeStruct((mem_b.shape[0], lw["w_kv_mem"].shape[1]), BF16),
        pl.BlockSpec((mem_b.shape[0], lw["w_kv_mem"].shape[1]), lambda j, i: (i, j)),
        mem_b.shape[0], lw["w_kv_mem"].shape[1], "proj_mem_kv")
    x, xb = _mem_attention(xb, x, lw["w_q_mem"], kv_mem, lw["w_o_mem"], lw["ln_g"][1:2], lw["ln_b"][1:2])
    x, xb = _ffn(xb, x, lw["w_ffn_in"], lw["w_ffn_out"], lw["ln_g"][2:3], lw["ln_b"][2:3])
    return (x, xb), None


def _prepare(S, D, w_in, b_gate, kv_norm_g, idx_k_norm_g, idx_k_norm_b, w_uk, w_uv, w_o_dsa, w_o_moba,
             w_out, w_q_mem, w_kv_mem, w_o_mem, w_ffn_in, w_ffn_out, ln_g, ln_b):
    L = w_in.shape[0]
    NB = S // MOBA_BLOCK
    assert S % 512 == 0 and S >= 1024 and NB + 2 <= HEAD_DIM
    slopes = _alibi_slopes(N_HEADS_MOBA)
    assert all(float(np.float32(s).astype(BF16)) == s for s in slopes)
    slopes_moba = jnp.asarray(slopes, F32)

    pos = jnp.arange(S, dtype=jnp.int32)
    lane = jnp.arange(HEAD_DIM, dtype=jnp.int32)[None, :]
    tail = jnp.where(lane == (pos // MOBA_BLOCK)[:, None], 1.0, 0.0)
    tail = jnp.where(lane == NB, (pos // LANES)[:, None].astype(F32), tail)
    tail = jnp.where(lane == NB + 1, (pos % LANES)[:, None].astype(F32), tail)
    kaug_tail = tail.astype(BF16)

    idx_pad = LANES - IDX_DIM
    weights = {
        "w_in": w_in.astype(BF16),
        "b_gate": b_gate.reshape(L, 1, 2 * D),
        "kv_norm_g": kv_norm_g.reshape(L, 1, KV_LATENT),
        "idx_g": jnp.pad(idx_k_norm_g, ((0, 0), (0, idx_pad))).reshape(L, 1, LANES),
        "idx_b": jnp.pad(idx_k_norm_b, ((0, 0), (0, idx_pad))).reshape(L, 1, LANES),
        "w_uk": w_uk.astype(BF16),
        "w_uv": w_uv.astype(BF16),
        "w_o_dsa": w_o_dsa.astype(BF16),
        "w_o_moba": w_o_moba.astype(BF16),
        "w_out": w_out.astype(BF16),
        "w_q_mem": w_q_mem.astype(BF16),
        "w_kv_mem": w_kv_mem.astype(BF16),
        "w_o_mem": w_o_mem.astype(BF16),
        "w_ffn_in": w_ffn_in.astype(BF16),
        "w_ffn_out": w_ffn_out.astype(BF16),
        "ln_g": ln_g,
        "ln_b": ln_b,
    }
    return weights, kaug_tail, slopes_moba


def kernel(x, mem, w_in, b_gate, kv_norm_g, idx_k_norm_g, idx_k_norm_b, w_uk, w_uv, w_o_dsa, w_o_moba,
           w_out, w_q_mem, w_kv_mem, w_o_mem, w_ffn_in, w_ffn_out, ln_g, ln_b):
    B, S, D = x.shape
    weights, kaug_tail, slopes_moba = _prepare(
        S, D, w_in, b_gate, kv_norm_g, idx_k_norm_g, idx_k_norm_b, w_uk, w_uv, w_o_dsa, w_o_moba,
        w_out, w_q_mem, w_kv_mem, w_o_mem, w_ffn_in, w_ffn_out, ln_g, ln_b)
    outs = []
    for b in range(B):
        xb0 = x[b]
        mem_b = mem[b].astype(BF16)
        step = functools.partial(_layer, mem_b=mem_b, kaug_tail=kaug_tail, slopes_moba=slopes_moba)
        (xf, _), _ = lax.scan(step, (xb0, xb0.astype(BF16)), weights)
        outs.append(xf)
    return jnp.stack(outs, axis=0)
```

```python
import functools
import math

import numpy as np
import jax
import jax.numpy as jnp
from jax import lax
from jax.experimental import pallas as pl
from jax.experimental.pallas import tpu as pltpu

HEAD_DIM = 128
N_HEADS_DSA = 8
N_HEADS_MOBA = 8
KV_LATENT = 256
IDX_HEADS = 16
IDX_DIM = 64
DSA_TOPK_MAX = 256
MOBA_BLOCK = 256
MOBA_TOPK_MAX = 3
N_HEADS_MEM = 4
TRUNK_DEPTH = 4
ALPHA = (2.0 * TRUNK_DEPTH) ** 0.25
LN_EPS = 1e-5

LOG2E = math.log2(math.e)
INT_MIN = -(2 ** 31)
NEG_BIG = -1e30
MOBA_MASK = -float(2 ** 30)
SLOPE_TERMS = 3
VMEM_LIMIT_V7X = 56 * 1024 * 1024
LANES = 128

F32 = jnp.float32
BF16 = jnp.bfloat16


def _alibi_slopes(n):
    return [2.0 ** (-8.0 * (i + 1) / n) for i in range(n)]


def _params(sem):
    return pltpu.CompilerParams(dimension_semantics=sem, vmem_limit_bytes=VMEM_LIMIT_V7X)


def _resident(shape):
    nd = len(shape)
    return pl.BlockSpec(shape, lambda *_: (0,) * nd, pipeline_mode=pl.Buffered(1))


def _layer_norm(z, g, b):
    mu = jnp.mean(z, axis=-1, keepdims=True)
    zc = z - mu
    var = jnp.mean(zc * zc, axis=-1, keepdims=True)
    return zc * lax.rsqrt(var + LN_EPS) * g + b


def _dot(a, b):
    return jnp.dot(a, b, preferred_element_type=F32)


def _dot_nt(a, b):
    return lax.dot_general(a, b, (((1,), (1,)), ((), ())), preferred_element_type=F32)


def _proj_call(body, x, w, extras, extra_specs, out_shape, out_specs, tm, tn, name):
    m, k = x.shape
    n = w.shape[1]
    grid = (n // tn, m // tm)
    in_specs = [pl.BlockSpec((tm, k), lambda j, i: (i, 0)),
                pl.BlockSpec((k, tn), lambda j, i: (0, j))] + list(extra_specs)
    return pl.pallas_call(
        body, grid=grid, in_specs=in_specs, out_specs=out_specs, out_shape=out_shape,
        compiler_params=_params(("arbitrary", "arbitrary")), name=name,
    )(x, w, *extras)


def _qlat_body(x_ref, w_ref, wuk_ref, o_ref):
    acc = _dot(x_ref[...], w_ref[...])
    scale = HEAD_DIM ** -0.5 * LOG2E
    for h in range(N_HEADS_DSA):
        qh = acc[:, h * HEAD_DIM:(h + 1) * HEAD_DIM].astype(BF16)
        o_ref[h] = (_dot(qh, wuk_ref[h]) * scale).astype(o_ref.dtype)


def _ckv_body(x_ref, w_ref, g_ref, o_ref):
    acc = _dot(x_ref[...], w_ref[...])
    ms = jnp.mean(acc * acc, axis=-1, keepdims=True)
    o_ref[...] = (acc * lax.rsqrt(ms + LN_EPS) * g_ref[...]).astype(o_ref.dtype)


def _qidx_body(x_ref, w_ref, o_ref):
    acc = _dot(x_ref[...], w_ref[...])
    for g in range(IDX_HEADS // 2):
        o_ref[g] = acc[:, g * LANES:(g + 1) * LANES].astype(o_ref.dtype)


def _kw_body(x_ref, w_ref, g_ref, b_ref, o_ref):
    acc = _dot(x_ref[...], w_ref[...])
    lane = lax.broadcasted_iota(jnp.int32, acc.shape, 1)
    is_k = lane < IDX_DIM
    mu = jnp.sum(jnp.where(is_k, acc, 0.0), axis=-1, keepdims=True) * (1.0 / IDX_DIM)
    zc = jnp.where(is_k, acc - mu, 0.0)
    var = jnp.sum(zc * zc, axis=-1, keepdims=True) * (1.0 / IDX_DIM)
    kn = zc * lax.rsqrt(var + LN_EPS) * g_ref[...] + b_ref[...]
    w_scale = IDX_DIM ** -0.5 * IDX_HEADS ** -0.5
    o_ref[...] = jnp.where(is_k, kn, acc * w_scale)


def _qkv_body(x_ref, w_ref, o_ref):
    acc = _dot(x_ref[...], w_ref[...])
    scale = jnp.where(pl.program_id(0) == 0, HEAD_DIM ** -0.5 * LOG2E, 1.0).astype(F32)
    o_ref[...] = (acc * scale).astype(o_ref.dtype)


def _gate_body(x_ref, w_ref, b_ref, o_ref):
    acc = _dot(x_ref[...], w_ref[...]) + b_ref[...]
    o_ref[...] = jax.nn.sigmoid(acc).astype(o_ref.dtype)


def _plain_body(x_ref, w_ref, o_ref):
    o_ref[...] = _dot(x_ref[...], w_ref[...]).astype(o_ref.dtype)


def _two_slot_loop(n, step):
    def pair(t, carry):
        step(2 * t, 0)
        step(2 * t + 1, 1)
        return carry

    lax.fori_loop(0, n // 2, pair, 0)

    @pl.when(n % 2 == 1)
    def _():
        step(n - 1, 0)


def _dsa_kernel(qlat_ref, qidx_ref, kw_ref, ckv_ref, kte_ref, kto_ref, wuv_ref, o_ref,
                keys_ref, acc_ref, m_ref, l_ref, sa_ref, p_ref, alpha_ref, *, T, KS, S, topk):
    H = N_HEADS_DSA
    R = H * T
    G = IDX_HEADS // 2
    nsub = KS // LANES
    i = pl.program_id(0)
    q0 = i * T
    n_chunks = (q0 + T + KS - 1) // KS
    int_min = jnp.int32(INT_MIN)

    def chunk_offset(c):
        return pl.multiple_of(jnp.minimum(c, n_chunks - 1) * KS, KS)

    qi = qidx_ref[...].reshape(G * T, LANES)
    row_ks = lax.broadcasted_iota(jnp.int32, (T, KS), 0)
    lane_ks = lax.broadcasted_iota(jnp.int32, (T, KS), 1)

    n_wide = (n_chunks + 1) // 2

    def phase_a(c, carry):
        for half in range(2):
            off = pl.multiple_of(c * (2 * KS) + half * KS, KS)
            le = _dot(qi, kte_ref[:, pl.ds(off, KS)])
            lo = _dot(qi, kto_ref[:, pl.ds(off, KS)])
            sc = jnp.zeros((T, KS), F32)
            for g in range(G):
                we = kw_ref[:, IDX_DIM + 2 * g:IDX_DIM + 2 * g + 1]
                wo = kw_ref[:, IDX_DIM + 2 * g + 1:IDX_DIM + 2 * g + 2]
                sc = sc + we * jnp.maximum(le[g * T:(g + 1) * T], 0.0)
                sc = sc + wo * jnp.maximum(lo[g * T:(g + 1) * T], 0.0)
            bits = pltpu.bitcast(sc, jnp.int32)
            key = bits ^ ((bits >> 31) & jnp.int32(0x7FFFFFFF))
            key = jnp.where(off + lane_ks <= q0 + row_ks, key, int_min)
            keys_ref[:, pl.ds(off, KS)] = key
        return carry

    lax.fori_loop(0, n_wide, phase_a, 0)

    def count_ge(cand):
        def body(c, cnt):
            off = pl.multiple_of(c * (2 * KS), 2 * KS)
            k = keys_ref[:, pl.ds(off, 2 * KS)]
            for a in range(2 * nsub):
                cnt = cnt + jnp.where(k[:, a * LANES:(a + 1) * LANES] >= cand, 1.0, 0.0)
            return cnt
        cnt = lax.fori_loop(0, n_wide, body, jnp.zeros((T, LANES), F32))
        return jnp.broadcast_to(jnp.sum(cnt, axis=1, keepdims=True), (T, LANES))

    row_t = lax.broadcasted_iota(jnp.int32, (T, LANES), 0)
    few_keys = q0 + row_t + 1 <= topk

    def bis_cond(carry):
        b, _, _, unsettled = carry
        return jnp.logical_and(b < 32, unsettled > 0.0)

    def bis_step(carry):
        b, u, cnt_u, _ = carry
        u_try = u | jnp.left_shift(jnp.int32(1), 31 - b)
        c = count_ge(u_try ^ int_min)
        take = c >= float(topk)
        u = jnp.where(take, u_try, u)
        cnt_u = jnp.where(take, c, cnt_u)
        settled = jnp.logical_or(cnt_u == float(topk), few_keys)
        return b + 1, u, cnt_u, jnp.max(jnp.where(settled, 0.0, 1.0))

    u0 = jnp.zeros((T, LANES), jnp.int32)
    c0 = jnp.full((T, LANES), float(S), F32)
    _, u, cnt_u, _ = lax.while_loop(bis_cond, bis_step, (jnp.int32(0), u0, c0, jnp.float32(1.0)))
    tau = u ^ int_min
    excess = jnp.logical_and(cnt_u > float(topk), u != 0)
    any_excess = jnp.max(jnp.where(excess, 1.0, 0.0)) > 0.0

    @pl.when(any_excess)
    def _break_ties():
        need = float(topk) - count_ge(tau + 1)
        lane_sub = lax.broadcasted_iota(jnp.int32, (T, LANES), 1)

        def count_tie_below(jt):
            def body(c, cnt):
                off = pl.multiple_of(c * KS, KS)
                k = keys_ref[:, pl.ds(off, KS)]
                for a in range(nsub):
                    hit = jnp.logical_and(k[:, a * LANES:(a + 1) * LANES] == tau,
                                          off + a * LANES + lane_sub < jt)
                    cnt = cnt + jnp.where(hit, 1.0, 0.0)
                return cnt
            cnt = lax.fori_loop(0, n_chunks, body, jnp.zeros((T, LANES), F32))
            return jnp.broadcast_to(jnp.sum(cnt, axis=1, keepdims=True), (T, LANES))

        nbits = max(1, (S - 1).bit_length())

        def j_step(b, j):
            j_try = j | jnp.left_shift(jnp.int32(1), nbits - 1 - b)
            return jnp.where(count_tie_below(j_try) < need, j_try, j)

        j_keep = lax.fori_loop(0, nbits, j_step, jnp.zeros((T, LANES), jnp.int32))

        def demote(c, carry):
            off = pl.multiple_of(c * KS, KS)
            for a in range(nsub):
                sl = pl.ds(off + a * LANES, LANES)
                k = keys_ref[:, sl]
                drop = jnp.logical_and(jnp.logical_and(k == tau, excess),
                                       off + a * LANES + lane_sub > j_keep)
                keys_ref[:, sl] = jnp.where(drop, tau - 1, k)
            return carry

        lax.fori_loop(0, n_chunks, demote, 0)

    tau_sel = jnp.maximum(tau, int_min + 1)

    ql = qlat_ref[...].reshape(R, KV_LATENT)
    m_ref[...] = jnp.full((R, LANES), NEG_BIG, F32)
    l_ref[...] = jnp.zeros((R, LANES), F32)
    acc_ref[...] = jnp.zeros((R, KV_LATENT), F32)
    slopes = _alibi_slopes(H)
    lane_row = lax.broadcasted_iota(jnp.int32, (1, KS), 1)
    tau_ks = jnp.concatenate([tau_sel] * nsub, axis=1)

    def scores_into(slot, c):
        sa_ref[slot] = _dot_nt(ql, ckv_ref[pl.ds(chunk_offset(c), KS), :])

    def accumulate(slot, c):
        alpha = alpha_ref[slot]
        alpha_kv = jnp.concatenate([alpha] * (KV_LATENT // LANES), axis=1)
        kv = ckv_ref[pl.ds(chunk_offset(c), KS), :]
        acc_ref[...] = acc_ref[...] * alpha_kv + _dot(p_ref[slot], kv)

    def phase_c(c, slot):
        accumulate(1 - slot, jnp.maximum(c - 1, 0))
        off = chunk_offset(c)
        mb = jnp.where(keys_ref[:, pl.ds(off, KS)] >= tau_ks, 0.0, NEG_BIG)
        rel = (off + lane_row - q0).astype(F32)
        bias = jnp.concatenate([mb + (slopes[h] * LOG2E) * rel for h in range(H)], axis=0)
        sb = sa_ref[slot] + bias
        m_prev = m_ref[...]
        m_new = jnp.maximum(m_prev, jnp.max(sb, axis=1, keepdims=True))
        alpha = jnp.exp2(m_prev - m_new)
        p = jnp.exp2(sb - jnp.concatenate([m_new] * nsub, axis=1))
        psum = p[:, 0:LANES]
        for a in range(1, nsub):
            psum = psum + p[:, a * LANES:(a + 1) * LANES]
        l_ref[...] = alpha * l_ref[...] + psum
        m_ref[...] = m_new
        alpha_ref[slot] = alpha
        p_ref[slot] = p.astype(BF16)
        scores_into(1 - slot, c + 1)

    p_ref[1] = jnp.zeros((R, KS), BF16)
    alpha_ref[1] = jnp.ones((R, LANES), F32)
    scores_into(0, 0)
    _two_slot_loop(n_chunks, phase_c)
    last = n_chunks - 1

    @pl.when(last % 2 == 0)
    def _():
        accumulate(0, last)

    @pl.when(last % 2 == 1)
    def _():
        accumulate(1, last)

    l_fin = jnp.sum(l_ref[...], axis=1, keepdims=True)
    for h in range(H):
        rows = slice(h * T, (h + 1) * T)
        oh = (acc_ref[rows] / l_fin[rows]).astype(BF16)
        o_ref[:, h * HEAD_DIM:(h + 1) * HEAD_DIM] = _dot(oh, wuv_ref[h]).astype(o_ref.dtype)


def _dsa_attention(qlat, qidx, kw, ckv, kte, kto, wuv):
    H = N_HEADS_DSA
    S = ckv.shape[0]
    T, KS = 128, 512
    topk = min(DSA_TOPK_MAX, S // 4)
    assert S % (2 * KS) == 0
    R = H * T
    kern = functools.partial(_dsa_kernel, T=T, KS=KS, S=S, topk=topk)
    return pl.pallas_call(
        kern, grid=(S // T,),
        in_specs=[pl.BlockSpec((H, T, KV_LATENT), lambda i: (0, i, 0)),
                  pl.BlockSpec((IDX_HEADS // 2, T, LANES), lambda i: (0, i, 0)),
                  pl.BlockSpec((T, LANES), lambda i: (i, 0)),
                  _resident(ckv.shape), _resident(kte.shape), _resident(kto.shape),
                  _resident(wuv.shape)],
        out_specs=pl.BlockSpec((T, H * HEAD_DIM), lambda i: (i, 0)),
        out_shape=jax.ShapeDtypeStruct((S, H * HEAD_DIM), BF16),
        scratch_shapes=[pltpu.VMEM((T, S), jnp.int32),
                        pltpu.VMEM((R, KV_LATENT), F32),
                        pltpu.VMEM((R, LANES), F32),
                        pltpu.VMEM((R, LANES), F32),
                        pltpu.VMEM((2, R, KS), F32),
                        pltpu.VMEM((2, R, KS), BF16),
                        pltpu.VMEM((2, R, LANES), F32)],
        compiler_params=_params(("arbitrary",)), name="dsa_attention",
    )(qlat, qidx, kw, ckv, kte, kto, wuv)


def _moba_kernel(slope_ref, q_ref, kaug_ref, v_ref, o_ref, kmean_ref, qaug_ref,
                 s_ref, p_ref, alpha_ref, m_ref, l_ref, acc_ref, *, Tq, S, n_sel):
    B = MOBA_BLOCK
    NB = S // B
    h = pl.program_id(0)
    i = pl.program_id(1)
    q0 = i * Tq

    @pl.when(i == 0)
    def _block_means():
        kmean_ref[...] = jnp.zeros(kmean_ref.shape, F32)
        for b in range(NB):
            kb = kaug_ref[b * B:(b + 1) * B, 0:HEAD_DIM].astype(F32)
            kmean_ref[b:b + 1, :] = jnp.mean(kb, axis=0, keepdims=True)

    q = q_ref[...]
    gate = _dot_nt(q, kmean_ref[...].astype(BF16))
    row = lax.broadcasted_iota(jnp.int32, (Tq, LANES), 0)
    lane = lax.broadcasted_iota(jnp.int32, (Tq, LANES), 1)
    lane_f = lane.astype(F32)
    own = lax.shift_right_arithmetic(q0 + row, jnp.int32(B.bit_length() - 1))
    neg_inf = jnp.float32(-jnp.inf)
    g = jnp.where(lane < own, gate, neg_inf)
    selb = jnp.full((Tq, LANES), MOBA_MASK, F32)
    for _ in range(n_sel):
        mx = jnp.max(g, axis=1, keepdims=True)
        first = jnp.min(jnp.where(g == mx, lane_f, float(LANES)), axis=1, keepdims=True)
        pick = lane_f == first
        selb = jnp.where(jnp.logical_and(pick, mx > neg_inf), 0.0, selb)
        g = jnp.where(pick, neg_inf, g)
    selb = jnp.where(lane == own, 0.0, selb)
    aug = jnp.where(lane < NB, selb, 0.0)
    for k in range(SLOPE_TERMS):
        term = slope_ref[h * SLOPE_TERMS + k]
        aug = jnp.where(lane == NB + k, term * LANES, aug)
        aug = jnp.where(lane == NB + SLOPE_TERMS + k, term, aug)
    qaug_ref[:, 0:HEAD_DIM] = q
    qaug_ref[:, HEAD_DIM:2 * HEAD_DIM] = aug.astype(BF16)

    row_b = lax.broadcasted_iota(jnp.int32, (Tq, B), 0)
    lane_b = lax.broadcasted_iota(jnp.int32, (Tq, B), 1)

    assert Tq == 2 * B

    def step(j, slot, masked, with_next):
        other = 1 - slot
        off_prev = pl.multiple_of(jnp.maximum(j - 1, 0) * B, B)
        acc_ref[...] = acc_ref[...] * alpha_ref[other] + _dot(p_ref[other], v_ref[pl.ds(off_prev, B), :])
        s = s_ref[slot]
        if masked:
            s = jnp.where(j * B + lane_b <= q0 + row_b, s, NEG_BIG)
        m_prev = m_ref[...]
        m_new = jnp.maximum(m_prev, jnp.max(s, axis=1, keepdims=True))
        alpha_ref[slot] = jnp.exp2(m_prev - m_new)
        p = jnp.exp2(s - jnp.concatenate([m_new] * (B // LANES), axis=1))
        psum = p[:, 0:LANES]
        for a in range(1, B // LANES):
            psum = psum + p[:, a * LANES:(a + 1) * LANES]
        l_ref[...] = alpha_ref[slot] * l_ref[...] + psum
        m_ref[...] = m_new
        p_ref[slot] = p.astype(BF16)
        if with_next:
            off_next = pl.multiple_of((j + 1) * B, B)
            s_ref[other] = _dot_nt(qaug_ref[...], kaug_ref[pl.ds(off_next, B), :])

    n_past = q0 // B
    acc_ref[...] = jnp.zeros(acc_ref.shape, F32)
    m_ref[...] = jnp.full(m_ref.shape, NEG_BIG, F32)
    l_ref[...] = jnp.zeros(l_ref.shape, F32)
    p_ref[1] = jnp.zeros((Tq, B), BF16)
    alpha_ref[1] = jnp.ones((Tq, LANES), F32)
    s_ref[0] = _dot_nt(qaug_ref[...], kaug_ref[0:B, :])

    def quad(t, carry):
        for d in range(4):
            step(4 * t + d, d % 2, False, True)
        return carry

    lax.fori_loop(0, n_past // 4, quad, 0)

    @pl.when(n_past % 4 == 2)
    def _():
        step(n_past - 2, 0, False, True)
        step(n_past - 1, 1, False, True)

    step(n_past, 0, True, True)
    step(n_past + 1, 1, True, False)
    off_last = pl.multiple_of((n_past + 1) * B, B)
    acc = acc_ref[...] * alpha_ref[1] + _dot(p_ref[1], v_ref[pl.ds(off_last, B), :])
    o_ref[...] = (acc / jnp.sum(l_ref[...], axis=1, keepdims=True)).astype(o_ref.dtype)


def _moba_attention(qkv, kaug, slopes):
    H = N_HEADS_MOBA
    S = qkv.shape[0]
    Tq = 512
    NB = S // MOBA_BLOCK
    n_sel = min(MOBA_TOPK_MAX, NB - 1)
    kern = functools.partial(_moba_kernel, Tq=Tq, S=S, n_sel=n_sel)
    return pl.pallas_call(
        kern, grid=(H, S // Tq),
        in_specs=[pl.BlockSpec(memory_space=pltpu.SMEM),
                  pl.BlockSpec((Tq, HEAD_DIM), lambda h, i: (i, h)),
                  pl.BlockSpec((S, 2 * HEAD_DIM), lambda h, i: (0, h)),
                  pl.BlockSpec((S, HEAD_DIM), lambda h, i: (0, 2 * H + h))],
        out_specs=pl.BlockSpec((Tq, HEAD_DIM), lambda h, i: (i, h)),
        out_shape=jax.ShapeDtypeStruct((S, H * HEAD_DIM), BF16),
        scratch_shapes=[pltpu.VMEM((LANES, HEAD_DIM), F32),
                        pltpu.VMEM((Tq, 2 * HEAD_DIM), BF16),
                        pltpu.VMEM((2, Tq, MOBA_BLOCK), F32),
                        pltpu.VMEM((2, Tq, MOBA_BLOCK), BF16),
                        pltpu.VMEM((2, Tq, LANES), F32),
                        pltpu.VMEM((Tq, LANES), F32),
                        pltpu.VMEM((Tq, LANES), F32),
                        pltpu.VMEM((Tq, HEAD_DIM), F32)],
        compiler_params=_params(("arbitrary", "arbitrary")), name="moba_attention",
    )(slopes, qkv, kaug, qkv)


def _merge_kernel(oa_ref, ob_ref, g_ref, x_ref, woa_ref, wob_ref, wout_ref, lng_ref, lnb_ref,
                  xo_ref, xbo_ref):
    d = x_ref.shape[1]
    a = _dot(oa_ref[...], woa_ref[...])
    b = _dot(ob_ref[...], wob_ref[...])
    merged = g_ref[:, 0:d].astype(F32) * a + g_ref[:, d:2 * d].astype(F32) * b
    y = _dot(merged.astype(BF16), wout_ref[...])
    xn = _layer_norm(ALPHA * x_ref[...] + y, lng_ref[...], lnb_ref[...])
    xo_ref[...] = xn
    xbo_ref[...] = xn.astype(BF16)


def _merge(oa, ob, gates, x, woa, wob, wout, lng, lnb):
    S, D = x.shape
    tm = 256
    row = lambda w: pl.BlockSpec((tm, w), lambda i: (i, 0))
    return pl.pallas_call(
        _merge_kernel, grid=(S // tm,),
        in_specs=[row(oa.shape[1]), row(ob.shape[1]), row(2 * D), row(D),
                  _resident(woa.shape), _resident(wob.shape), _resident(wout.shape),
                  _resident(lng.shape), _resident(lnb.shape)],
        out_specs=[row(D), row(D)],
        out_shape=[jax.ShapeDtypeStruct((S, D), F32), jax.ShapeDtypeStruct((S, D), BF16)],
        compiler_params=_params(("arbitrary",)), name="merge_out_ln",
    )(oa, ob, gates, x, woa, wob, wout, lng, lnb)


def _mem_kernel(xb_ref, x_ref, wq_ref, kv_ref, wo_ref, lng_ref, lnb_ref, xo_ref, xbo_ref):
    H = N_HEADS_MEM
    W = H * HEAD_DIM
    q = (_dot(xb_ref[...], wq_ref[...]) * (HEAD_DIM ** -0.5)).astype(BF16)
    outs = []
    for h in range(H):
        cols = slice(h * HEAD_DIM, (h + 1) * HEAD_DIM)
        s = _dot_nt(q[:, cols], kv_ref[:, cols])
        m = jnp.max(s, axis=1, keepdims=True)
        p = jnp.exp(s - m)
        l = jnp.sum(p, axis=1, keepdims=True)
        o = _dot(p.astype(BF16), kv_ref[:, W + h * HEAD_DIM:W + (h + 1) * HEAD_DIM])
        outs.append((o / l).astype(BF16))
    y = _dot(jnp.concatenate(outs, axis=1), wo_ref[...])
    xn = _layer_norm(ALPHA * x_ref[...] + y, lng_ref[...], lnb_ref[...])
    xo_ref[...] = xn
    xbo_ref[...] = xn.astype(BF16)


def _mem_attention(xb, x, wq, kv, wo, lng, lnb):
    S, D = x.shape
    tm = 512
    row = lambda w: pl.BlockSpec((tm, w), lambda i: (i, 0))
    return pl.pallas_call(
        _mem_kernel, grid=(S // tm,),
        in_specs=[row(D), row(D), _resident(wq.shape), _resident(kv.shape), _resident(wo.shape),
                  _resident(lng.shape), _resident(lnb.shape)],
        out_specs=[row(D), row(D)],
        out_shape=[jax.ShapeDtypeStruct((S, D), F32), jax.ShapeDtypeStruct((S, D), BF16)],
        compiler_params=_params(("arbitrary",)), name="mem_attn_ln",
    )(xb, x, wq, kv, wo, lng, lnb)


def _ffn_kernel(xb_ref, x_ref, wg_ref, wu_ref, wd_ref, lng_ref, lnb_ref, xo_ref, xbo_ref, acc_ref):
    c = pl.program_id(1)

    @pl.when(c == 0)
    def _():
        acc_ref[...] = jnp.zeros(acc_ref.shape, F32)

    xb = xb_ref[...]
    g = _dot(xb, wg_ref[...])
    u = _dot(xb, wu_ref[...])
    hidden = (g * jax.nn.sigmoid(g) * u).astype(BF16)
    acc_ref[...] += _dot(hidden, wd_ref[...])

    @pl.when(c == pl.num_programs(1) - 1)
    def _():
        xn = _layer_norm(ALPHA * x_ref[...] + acc_ref[...], lng_ref[...], lnb_ref[...])
        xo_ref[...] = xn
        xbo_ref[...] = xn.astype(BF16)


def _ffn(xb, x, w_in, w_out, lng, lnb):
    S, D = x.shape
    F = w_out.shape[0]
    tm, tf = 512, 512
    nf = F // tf
    row = lambda w: pl.BlockSpec((tm, w), lambda i, c: (i, 0))
    return pl.pallas_call(
        _ffn_kernel, grid=(S // tm, nf),
        in_specs=[row(D), row(D),
                  pl.BlockSpec((D, tf), lambda i, c: (0, c)),
                  pl.BlockSpec((D, tf), lambda i, c: (0, nf + c)),
                  pl.BlockSpec((tf, D), lambda i, c: (c, 0)),
                  _resident(lng.shape), _resident(lnb.shape)],
        out_specs=[row(D), row(D)],
        out_shape=[jax.ShapeDtypeStruct((S, D), F32), jax.ShapeDtypeStruct((S, D), BF16)],
        scratch_shapes=[pltpu.VMEM((tm, D), F32)],
        compiler_params=_params(("arbitrary", "arbitrary")), name="swiglu_ffn_ln",
    )(xb, x, w_in, w_in, w_out, lng, lnb)


def _mixer_branches(xb, lw, kaug_tail, slopes_moba):
    S, D = xb.shape
    H = N_HEADS_DSA
    dsa_w, moba_w = H * HEAD_DIM, N_HEADS_MOBA * HEAD_DIM
    idx_w = IDX_HEADS * IDX_DIM
    o_q = 0
    o_c = o_q + dsa_w
    o_qi = o_c + KV_LATENT
    o_ki = o_qi + idx_w
    o_qm = o_ki + IDX_DIM + IDX_HEADS
    o_ga = o_qm + 3 * moba_w
    w_in = lw["w_in"]
    tm = 512

    qlat = _proj_call(
        _qlat_body, xb, w_in[:, o_q:o_c], [lw["w_uk"]], [_resident(lw["w_uk"].shape)],
        jax.ShapeDtypeStruct((H, S, KV_LATENT), BF16),
        pl.BlockSpec((H, tm, KV_LATENT), lambda j, i: (0, i, 0)), tm, dsa_w, "proj_qlat")
    ckv = _proj_call(
        _ckv_body, xb, w_in[:, o_c:o_qi], [lw["kv_norm_g"]], [_resident(lw["kv_norm_g"].shape)],
        jax.ShapeDtypeStruct((S, KV_LATENT), BF16),
        pl.BlockSpec((tm, KV_LATENT), lambda j, i: (i, 0)), tm, KV_LATENT, "proj_ckv")
    qidx = _proj_call(
        _qidx_body, xb, w_in[:, o_qi:o_ki], [], [],
        jax.ShapeDtypeStruct((IDX_HEADS // 2, S, LANES), BF16),
        pl.BlockSpec((IDX_HEADS // 2, tm, LANES), lambda j, i: (0, i, 0)), tm, idx_w, "proj_qidx")
    kw_pad = LANES - IDX_DIM - IDX_HEADS
    w_kw = jnp.pad(w_in[:, o_ki:o_qm], ((0, 0), (0, kw_pad)))
    kw = _proj_call(
        _kw_body, xb, w_kw, [lw["idx_g"], lw["idx_b"]],
        [_resident(lw["idx_g"].shape), _resident(lw["idx_b"].shape)],
        jax.ShapeDtypeStruct((S, LANES), F32),
        pl.BlockSpec((tm, LANES), lambda j, i: (i, 0)), tm, LANES, "proj_kidx_widx")
    qkv = _proj_call(
        _qkv_body, xb, w_in[:, o_qm:o_ga], [], [],
        jax.ShapeDtypeStruct((S, 3 * moba_w), BF16),
        pl.BlockSpec((tm, moba_w), lambda j, i: (i, j)), tm, moba_w, "proj_qkv_moba")
    gates = _proj_call(
        _gate_body, xb, w_in[:, o_ga:], [lw["b_gate"]], [pl.BlockSpec((1, 1024), lambda j, i: (0, j))],
        jax.ShapeDtypeStruct((S, 2 * D), BF16),
        pl.BlockSpec((tm, 1024), lambda j, i: (i, j)), tm, 1024, "proj_gates")

    kt = kw[:, :IDX_DIM].T.astype(BF16)
    zeros = jnp.zeros_like(kt)
    kte = jnp.concatenate([kt, zeros], axis=0)
    kto = jnp.concatenate([zeros, kt], axis=0)
    o_a = _dsa_attention(qlat, qidx, kw, ckv, kte, kto, lw["w_uv"])

    k3 = qkv[:, moba_w:2 * moba_w].reshape(S, N_HEADS_MOBA, HEAD_DIM)
    kaug = jnp.concatenate(
        [k3, jnp.broadcast_to(kaug_tail[:, None, :], (S, N_HEADS_MOBA, HEAD_DIM))], axis=-1
    ).reshape(S, N_HEADS_MOBA * 2 * HEAD_DIM)
    o_b = _moba_attention(qkv, kaug, slopes_moba)
    return o_a, o_b, gates


def _layer(carry, lw, mem_b, kaug_tail, slopes_moba):
    x, xb = carry
    o_a, o_b, gates = _mixer_branches(xb, lw, kaug_tail, slopes_moba)
    x, xb = _merge(o_a, o_b, gates, x, lw["w_o_dsa"], lw["w_o_moba"], lw["w_out"],
                   lw["ln_g"][0:1], lw["ln_b"][0:1])

    kv_mem = _proj_call(
        _plain_body, mem_b, lw["w_kv_mem"], [], [],
        jax.ShapeDtypeStruct((mem_b.shape[0], lw["w_kv_mem"].shape[1]), BF16),
        pl.BlockSpec((mem_b.shape[0], lw["w_kv_mem"].shape[1]), lambda j, i: (i, j)),
        mem_b.shape[0], lw["w_kv_mem"].shape[1], "proj_mem_kv")
    x, xb = _mem_attention(xb, x, lw["w_q_mem"], kv_mem, lw["w_o_mem"], lw["ln_g"][1:2], lw["ln_b"][1:2])
    x, xb = _ffn(xb, x, lw["w_ffn_in"], lw["w_ffn_out"], lw["ln_g"][2:3], lw["ln_b"][2:3])
    return (x, xb), None


def _prepare(S, D, w_in, b_gate, kv_norm_g, idx_k_norm_g, idx_k_norm_b, w_uk, w_uv, w_o_dsa, w_o_moba,
             w_out, w_q_mem, w_kv_mem, w_o_mem, w_ffn_in, w_ffn_out, ln_g, ln_b):
    L = w_in.shape[0]
    NB = S // MOBA_BLOCK
    assert S % 512 == 0 and S >= 1024 and NB + 2 * SLOPE_TERMS <= HEAD_DIM
    terms = []
    for s in _alibi_slopes(N_HEADS_MOBA):
        rest = np.float64(s * LOG2E)
        for _ in range(SLOPE_TERMS):
            t = np.float64(np.float32(rest).astype(BF16))
            terms.append(float(t))
            rest = rest - t
    slopes_moba = jnp.asarray(terms, F32)

    pos = jnp.arange(S, dtype=jnp.int32)
    lane = jnp.arange(HEAD_DIM, dtype=jnp.int32)[None, :]
    tail = jnp.where(lane == (pos // MOBA_BLOCK)[:, None], 1.0, 0.0)
    is_hi = jnp.logical_and(lane >= NB, lane < NB + SLOPE_TERMS)
    is_lo = jnp.logical_and(lane >= NB + SLOPE_TERMS, lane < NB + 2 * SLOPE_TERMS)
    tail = jnp.where(is_hi, (pos // LANES)[:, None].astype(F32), tail)
    tail = jnp.where(is_lo, (pos % LANES)[:, None].astype(F32), tail)
    kaug_tail = tail.astype(BF16)

    idx_pad = LANES - IDX_DIM
    weights = {
        "w_in": w_in.astype(BF16),
        "b_gate": b_gate.reshape(L, 1, 2 * D),
        "kv_norm_g": kv_norm_g.reshape(L, 1, KV_LATENT),
        "idx_g": jnp.pad(idx_k_norm_g, ((0, 0), (0, idx_pad))).reshape(L, 1, LANES),
        "idx_b": jnp.pad(idx_k_norm_b, ((0, 0), (0, idx_pad))).reshape(L, 1, LANES),
        "w_uk": w_uk.astype(BF16),
        "w_uv": w_uv.astype(BF16),
        "w_o_dsa": w_o_dsa.astype(BF16),
        "w_o_moba": w_o_moba.astype(BF16),
        "w_out": w_out.astype(BF16),
        "w_q_mem": w_q_mem.astype(BF16),
        "w_kv_mem": w_kv_mem.astype(BF16),
        "w_o_mem": w_o_mem.astype(BF16),
        "w_ffn_in": w_ffn_in.astype(BF16),
        "w_ffn_out": w_ffn_out.astype(BF16),
        "ln_g": ln_g,
        "ln_b": ln_b,
    }
    return weights, kaug_tail, slopes_moba


def kernel(x, mem, w_in, b_gate, kv_norm_g, idx_k_norm_g, idx_k_norm_b, w_uk, w_uv, w_o_dsa, w_o_moba,
           w_out, w_q_mem, w_kv_mem, w_o_mem, w_ffn_in, w_ffn_out, ln_g, ln_b):
    B, S, D = x.shape
    weights, kaug_tail, slopes_moba = _prepare(
        S, D, w_in, b_gate, kv_norm_g, idx_k_norm_g, idx_k_norm_b, w_uk, w_uv, w_o_dsa, w_o_moba,
        w_out, w_q_mem, w_kv_mem, w_o_mem, w_ffn_in, w_ffn_out, ln_g, ln_b)
    outs = []
    for b in range(B):
        xb0 = x[b]
        mem_b = mem[b].astype(BF16)
        step = functools.partial(_layer, mem_b=mem_b, kaug_tail=kaug_tail, slopes_moba=slopes_moba)
        (xf, _), _ = lax.scan(step, (xb0, xb0.astype(BF16)), weights)
        outs.append(xf)
    return jnp.stack(outs, axis=0)
```
